```python
import jax, jax.numpy as jnp
from jax import lax
import numpy as np

D_MODEL = 1024
BATCH = 8
SEQ = 2048
DEPTH = 4
DEC_BATCH = 1
DEC_SEQ = 16384
PAST_LEN = 128

MLA_HEADS = 8
MLA_NOPE = 64
MLA_ROPE = 32
MLA_V = 64
Q_LORA = 256
KV_LORA = 128
GQA_HEADS = 8
GQA_KV_HEADS = 2
GQA_DIM = 64
WINDOW = 128
BLOCK = 128
D_FF = 2816
ROPE_THETA = 10000.0
EPS = 1e-6
N_BRANCH = 2
NEG_INF = -1e30
IN_COLS = (Q_LORA + KV_LORA + MLA_ROPE + GQA_HEADS * GQA_DIM
           + 2 * GQA_KV_HEADS * GQA_DIM + N_BRANCH * D_MODEL)

kernel_name = "hybrid_mla_wingqa_macaron_encoder"


def rmsnorm(x, g):
    xf = x.astype(jnp.float32)
    y = xf * lax.rsqrt(jnp.mean(xf * xf, axis=-1, keepdims=True) + EPS)
    return (y * g.astype(jnp.float32)).astype(x.dtype)


def rope(x, seq_len):
    d = x.shape[-1]
    freqs = ROPE_THETA ** (-jnp.arange(0, d, 2, dtype=jnp.float32) / d)
    ang = jnp.arange(seq_len, dtype=jnp.float32)[:, None] * freqs[None, :]
    cos = jnp.cos(ang)[None, :, None, :].astype(x.dtype)
    sin = jnp.sin(ang)[None, :, None, :].astype(x.dtype)
    x1, x2 = x[..., : d // 2], x[..., d // 2:]
    return jnp.concatenate([x1 * cos - x2 * sin, x2 * cos + x1 * sin], axis=-1)


def swiglu(x, w1, w3, w2):
    return (jax.nn.silu(x @ w1) * (x @ w3)) @ w2


def mla_attention(c_q, c_kv, k_pe_raw, q_norm, kv_norm, w_uq, w_ukv):
    B, S, _ = c_q.shape
    H = MLA_HEADS
    q = (rmsnorm(c_q, q_norm) @ w_uq).reshape(B, S, H, MLA_NOPE + MLA_ROPE)
    q = jnp.concatenate([q[..., :MLA_NOPE], rope(q[..., MLA_NOPE:], S)], axis=-1)
    kv = (rmsnorm(c_kv, kv_norm) @ w_ukv).reshape(B, S, H, MLA_NOPE + MLA_V)
    k_nope, v = kv[..., :MLA_NOPE], kv[..., MLA_NOPE:]
    k_pe = rope(k_pe_raw[:, :, None, :], S)
    k = jnp.concatenate([k_nope, jnp.broadcast_to(k_pe, (B, S, H, MLA_ROPE))], axis=-1)
    scale = (MLA_NOPE + MLA_ROPE) ** -0.5
    nb = S // BLOCK
    qb = q.reshape(B, nb, BLOCK, H, MLA_NOPE + MLA_ROPE).transpose(1, 0, 2, 3, 4)

    def one_block(qi):
        s = jnp.einsum('bqhd,bkhd->bhqk', qi, k).astype(jnp.float32) * scale
        p = jax.nn.softmax(s, axis=-1).astype(v.dtype)
        return jnp.einsum('bhqk,bkhv->bqhv', p, v)

    o = lax.map(one_block, qb)
    return o.transpose(1, 0, 2, 3, 4).reshape(B, S, H * MLA_V)


def windowed_gqa(q, k, v, sink):
    B, S, H, d = q.shape
    KV = GQA_KV_HEADS
    G = H // KV
    nb = S // BLOCK
    q = rope(q, S)
    k = rope(k, S)
    pad = ((0, 0), (BLOCK, BLOCK), (0, 0), (0, 0))
    kb = jnp.pad(k, pad).reshape(B, nb + 2, BLOCK, KV, d)
    vb = jnp.pad(v, pad).reshape(B, nb + 2, BLOCK, KV, d)
    kw = jnp.concatenate([kb[:, :-2], kb[:, 1:-1], kb[:, 2:]], axis=2)
    vw = jnp.concatenate([vb[:, :-2], vb[:, 1:-1], vb[:, 2:]], axis=2)
    qb = q.reshape(B, nb, BLOCK, KV, G, d)
    s = jnp.einsum('bnqkgd,bnpkd->bnkgqp', qb, kw).astype(jnp.float32) * (d ** -0.5)
    blk = jnp.arange(nb)[:, None, None] * BLOCK
    qpos = blk + jnp.arange(BLOCK)[None, :, None]
    kpos = blk - BLOCK + jnp.arange(3 * BLOCK)[None, None, :]
    valid = (jnp.abs(qpos - kpos) <= WINDOW) & (kpos >= 0) & (kpos < S)
    s = jnp.where(valid[None, :, None, None, :, :], s, NEG_INF)
    sink_l = jnp.broadcast_to(sink.astype(jnp.float32).reshape(1, 1, KV, G, 1, 1),
                              s.shape[:-1] + (1,))
    p = jax.nn.softmax(jnp.concatenate([s, sink_l], axis=-1), axis=-1)[..., :-1]
    o = jnp.einsum('bnkgqp,bnpkd->bnqkgd', p.astype(v.dtype), vw)
    return o.reshape(B, S, H * d)


def setup_inputs(seed: int = 0) -> dict:
    key = jax.random.key(seed)
    ks = jax.random.split(key, 24)
    D = D_MODEL

    def w(k, shape, fan_in):
        return jax.random.normal(k, shape, jnp.float32) * (fan_in ** -0.5)

    def gain(k, shape):
        return 1.0 + 0.05 * jax.random.normal(k, shape, jnp.float32)

    return {
        "x_prompt": jax.random.normal(ks[0], (BATCH, SEQ, D), jnp.float32),
        "x_sample": jax.random.normal(ks[1], (DEC_BATCH, DEC_SEQ, D), jnp.float32),
        "ffn1_norm": gain(ks[2], (DEPTH, D)),
        "ffn1_w1": w(ks[3], (DEPTH, D, D_FF), D),
        "ffn1_w3": w(ks[4], (DEPTH, D, D_FF), D),
        "ffn1_w2": w(ks[5], (DEPTH, D_FF, D), D_FF),
        "mix_norm": gain(ks[6], (DEPTH, D)),
        "w_in": w(ks[7], (DEPTH, D, IN_COLS), D),
        "q_norm": gain(ks[8], (DEPTH, Q_LORA)),
        "kv_norm": gain(ks[9], (DEPTH, KV_LORA)),
        "w_uq": w(ks[10], (DEPTH, Q_LORA, MLA_HEADS * (MLA_NOPE + MLA_ROPE)), Q_LORA),
        "w_ukv": w(ks[11], (DEPTH, KV_LORA, MLA_HEADS * (MLA_NOPE + MLA_V)), KV_LORA),
        "w_o_mla": w(ks[12], (DEPTH, MLA_HEADS * MLA_V, D), MLA_HEADS * MLA_V),
        "sink": 0.5 * jax.random.normal(ks[13], (DEPTH, GQA_HEADS), jnp.float32),
        "w_o_gqa": w(ks[14], (DEPTH, GQA_HEADS * GQA_DIM, D), GQA_HEADS * GQA_DIM),
        "w_out": w(ks[15], (DEPTH, D, D), D),
        "ffn2_norm": gain(ks[16], (DEPTH, D)),
        "ffn2_w1": w(ks[17], (DEPTH, D, D_FF), D),
        "ffn2_w3": w(ks[18], (DEPTH, D, D_FF), D),
        "ffn2_w2": w(ks[19], (DEPTH, D_FF, D), D_FF),
        "final_norm": gain(ks[20], (D,)),
    }


def reference(x_prompt, x_sample, ffn1_norm, ffn1_w1, ffn1_w3, ffn1_w2, mix_norm, w_in,
              q_norm, kv_norm, w_uq, w_ukv, w_o_mla, sink, w_o_gqa, w_out,
              ffn2_norm, ffn2_w1, ffn2_w3, ffn2_w2, final_norm):
    sizes = [Q_LORA, KV_LORA, MLA_ROPE, GQA_HEADS * GQA_DIM,
             GQA_KV_HEADS * GQA_DIM, GQA_KV_HEADS * GQA_DIM, D_MODEL, D_MODEL]
    cuts = [int(c) for c in np.cumsum(sizes)[:-1]]

    def layer(x, l):
        B, S, _ = x.shape
        x = x + 0.5 * swiglu(rmsnorm(x, ffn1_norm[l]), ffn1_w1[l], ffn1_w3[l], ffn1_w2[l])
        u = rmsnorm(x, mix_norm[l])
        c_q, c_kv, k_pe, gq, gk, gv, g_a, g_b = jnp.split(u @ w_in[l], cuts, axis=-1)
        a = mla_attention(c_q, c_kv, k_pe, q_norm[l], kv_norm[l], w_uq[l], w_ukv[l]) @ w_o_mla[l]
        b = windowed_gqa(gq.reshape(B, S, GQA_HEADS, GQA_DIM),
                         gk.reshape(B, S, GQA_KV_HEADS, GQA_DIM),
                         gv.reshape(B, S, GQA_KV_HEADS, GQA_DIM), sink[l]) @ w_o_gqa[l]
        merged = jax.nn.sigmoid(g_a) * a + jax.nn.sigmoid(g_b) * b
        x = x + merged @ w_out[l]
        x = x + 0.5 * swiglu(rmsnorm(x, ffn2_norm[l]), ffn2_w1[l], ffn2_w3[l], ffn2_w2[l])
        return x

    def trunk(x):
        for l in range(DEPTH):
            x = layer(x, l)
        return rmsnorm(x, final_norm)

    y_prompt = trunk(x_prompt)
    y_sample = trunk(x_sample)
    return (y_prompt, y_sample)
```

```python
import functools

import jax
import jax.numpy as jnp
import numpy as np
from jax import lax
from jax.experimental import pallas as pl
from jax.experimental.pallas import tpu as pltpu

D_MODEL = 1024
DEPTH = 4
MLA_HEADS = 8
MLA_NOPE = 64
MLA_ROPE = 32
MLA_V = 64
Q_LORA = 256
KV_LORA = 128
GQA_HEADS = 8
GQA_KV_HEADS = 2
GQA_DIM = 64
WINDOW = 128
D_FF = 2816
ROPE_THETA = 10000.0
EPS = 1e-6
NEG_INF = -1e30

MLA_SCALE = float((MLA_NOPE + MLA_ROPE) ** -0.5)
GQA_SCALE = float(GQA_DIM ** -0.5)

LANES = 128
V7X_VMEM_BYTES = 64 * 1024 * 1024
VMEM_LIMIT = V7X_VMEM_BYTES * 7 // 8

BF16 = jnp.bfloat16
F32 = jnp.float32

TOKEN_TILE = 512
FF_CHUNK = 256
MLA_Q_BLOCK = 512
MLA_KV_CHUNK = 1024
GQA_Q_BLOCK = 256

_C_CQ = 0
_C_CKV = _C_CQ + Q_LORA
_C_KPE = _C_CKV + KV_LORA
_C_GQ = _C_KPE + LANES
_C_GK = _C_GQ + GQA_HEADS * GQA_DIM
_C_GKSW = _C_GK + LANES
_C_GV = _C_GKSW + LANES
MIX_COLS = _C_GV + 4 * LANES


def _params(n_grid, **kw):
    return pltpu.CompilerParams(
        dimension_semantics=("arbitrary",) * n_grid,
        vmem_limit_bytes=VMEM_LIMIT, **kw)


def _rmsnorm(x, g):
    ms = jnp.mean(x * x, axis=-1, keepdims=True)
    return x * lax.rsqrt(ms + EPS) * g


def _dot(a, b):
    return jnp.dot(a, b, preferred_element_type=F32)


def _dot_nt(a, b):
    return lax.dot_general(a, b, (((1,), (1,)), ((), ())),
                           preferred_element_type=F32)


def _dot_tn(a, b):
    return lax.dot_general(a, b, (((0,), (0,)), ((), ())),
                           preferred_element_type=F32)


def _ffn_kernel(x_ref, g_ref, w1_ref, w3_ref, w2_ref, *rest, final):
    if final:
        fg_ref, o_ref = rest
    else:
        (o_ref,) = rest
    x = x_ref[...]
    xn = _rmsnorm(x, g_ref[...]).astype(BF16)
    acc = None
    for c in range(D_FF // FF_CHUNK):
        sl = slice(c * FF_CHUNK, (c + 1) * FF_CHUNK)
        h1 = _dot(xn, w1_ref[:, sl])
        h3 = _dot(xn, w3_ref[:, sl])
        g = (h1 * jax.nn.sigmoid(h1) * h3).astype(BF16)
        d = _dot(g, w2_ref[sl, :])
        acc = d if acc is None else acc + d
    y = x + 0.5 * acc
    if final:
        y = _rmsnorm(y, fg_ref[...])
    o_ref[...] = y


def _const_spec(shape):
    return pl.BlockSpec(shape, lambda *_: (0,) * len(shape))


def _ffn(x, g, w1, w3, w2, final_gain=None):
    t = x.shape[0]
    final = final_gain is not None
    tok = pl.BlockSpec((TOKEN_TILE, D_MODEL), lambda i: (i, 0))
    in_specs = [tok, _const_spec((1, D_MODEL)), _const_spec((D_MODEL, D_FF)),
                _const_spec((D_MODEL, D_FF)), _const_spec((D_FF, D_MODEL))]
    args = [x, g, w1, w3, w2]
    if final:
        in_specs.append(_const_spec((1, D_MODEL)))
        args.append(final_gain)
    return pl.pallas_call(
        functools.partial(_ffn_kernel, final=final),
        grid=(t // TOKEN_TILE,),
        in_specs=in_specs,
        out_specs=tok,
        out_shape=jax.ShapeDtypeStruct((t, D_MODEL), F32),
        compiler_params=_params(1),
        name="ffn_final" if final else "ffn",
    )(*args)


def _rope(xb, c, s1, s2, half):
    fwd = pltpu.roll(xb, LANES - half, axis=1)
    bwd = pltpu.roll(xb, half, axis=1)
    return xb * c + fwd * s1 + bwd * s2


def _mix_in_kernel(x_ref, g_ref, wmix_ref, qn_ref, kvn_ref, wuq_ref, wuk_ref,
                   wuvt_ref, tab_ref, q_ref, k_ref, vt_ref, gq_ref, gk_ref,
                   gksw_ref, gv_ref):
    u = _rmsnorm(x_ref[...], g_ref[...]).astype(BF16)
    proj = _dot(u, wmix_ref[...])
    cq = _rmsnorm(proj[:, _C_CQ:_C_CKV], qn_ref[...]).astype(BF16)
    ckv = _rmsnorm(proj[:, _C_CKV:_C_KPE], kvn_ref[...]).astype(BF16)
    cm, s1m, s2m = tab_ref[0], tab_ref[1], tab_ref[2]
    cg, s1g, s2g = tab_ref[3], tab_ref[4], tab_ref[5]

    q = _dot(cq, wuq_ref[...])
    knope = _dot(ckv, wuk_ref[...])
    kpe = _rope(proj[:, _C_KPE:_C_GQ], cm, s1m, s2m, MLA_ROPE // 2)
    for h in range(MLA_HEADS):
        sl = slice(h * LANES, (h + 1) * LANES)
        qh = _rope(q[:, sl], cm, s1m, s2m, MLA_ROPE // 2) * MLA_SCALE
        q_ref[:, sl] = qh.astype(BF16)
        k_ref[:, sl] = (knope[:, sl] + kpe).astype(BF16)
    vt_ref[...] = _dot_nt(wuvt_ref[...], ckv).astype(BF16)

    for j in range(GQA_HEADS * GQA_DIM // LANES):
        sl = slice(_C_GQ + j * LANES, _C_GQ + (j + 1) * LANES)
        gqj = _rope(proj[:, sl], cg, s1g, s2g, GQA_DIM // 2) * GQA_SCALE
        gq_ref[:, j * LANES:(j + 1) * LANES] = gqj.astype(BF16)
    gk_ref[...] = _rope(proj[:, _C_GK:_C_GKSW], cg, s1g, s2g,
                        GQA_DIM // 2).astype(BF16)
    gksw_ref[...] = _rope(proj[:, _C_GKSW:_C_GV], cg, s1g, s2g,
                          GQA_DIM // 2).astype(BF16)
    gv_ref[...] = proj[:, _C_GV:MIX_COLS].astype(BF16)


def _mix_in(x, lw, tables, seq):
    t = x.shape[0]
    tm = TOKEN_TILE
    nblk = seq // tm
    hp = MLA_HEADS * LANES
    tok = lambda w: pl.BlockSpec((tm, w), lambda i: (i, 0))
    in_specs = [
        tok(D_MODEL), _const_spec((1, D_MODEL)), _const_spec((D_MODEL, MIX_COLS)),
        _const_spec((1, Q_LORA)), _const_spec((1, KV_LORA)),
        _const_spec((Q_LORA, hp)), _const_spec((KV_LORA, hp)),
        _const_spec((MLA_HEADS * MLA_V, KV_LORA)),
        pl.BlockSpec((6, tm, LANES), lambda i: (0, i % nblk, 0)),
    ]
    out_specs = [
        tok(hp), tok(hp),
        pl.BlockSpec((MLA_HEADS * MLA_V, tm), lambda i: (0, i)),
        tok(GQA_HEADS * GQA_DIM), tok(LANES), tok(LANES), tok(4 * LANES),
    ]
    out_shape = [
        jax.ShapeDtypeStruct((t, hp), BF16),
        jax.ShapeDtypeStruct((t, hp), BF16),
        jax.ShapeDtypeStruct((MLA_HEADS * MLA_V, t), BF16),
        jax.ShapeDtypeStruct((t, GQA_HEADS * GQA_DIM), BF16),
        jax.ShapeDtypeStruct((t, LANES), BF16),
        jax.ShapeDtypeStruct((t, LANES), BF16),
        jax.ShapeDtypeStruct((t, 4 * LANES), BF16),
    ]
    return pl.pallas_call(
        _mix_in_kernel,
        grid=(t // tm,),
        in_specs=in_specs,
        out_specs=out_specs,
        out_shape=out_shape,
        compiler_params=_params(1),
        name="mix_in",
    )(x, lw["mix_norm"], lw["w_mix"], lw["q_norm"], lw["kv_norm"], lw["w_uq"],
      lw["w_uk"], lw["w_uvt"], tables)


def _mla_kernel(q_ref, k_ref, vt_ref, o_ref, *, n_chunks):
    q = q_ref[...]
    qb = q.shape[0]

    def body(c, carry):
        m, l, acc = carry
        off = pl.multiple_of(c * MLA_KV_CHUNK, MLA_KV_CHUNK)
        k = k_ref[pl.ds(off, MLA_KV_CHUNK), :]
        s = _dot_nt(k, q)
        m_new = jnp.maximum(m, jnp.max(s, axis=0, keepdims=True))
        alpha = jnp.exp(m - m_new)
        p = jnp.exp(s - m_new)
        l = alpha * l + jnp.sum(p, axis=0, keepdims=True)
        vt = vt_ref[:, pl.ds(off, MLA_KV_CHUNK)]
        acc = alpha * acc + _dot(vt, p.astype(BF16))
        return m_new, l, acc

    m0 = jnp.full((1, qb), NEG_INF, F32)
    l0 = jnp.zeros((1, qb), F32)
    acc0 = jnp.zeros((MLA_V, qb), F32)
    _, l, acc = lax.fori_loop(0, n_chunks, body, (m0, l0, acc0))
    o_ref[...] = (acc / l).astype(BF16)


def _mla_attention(q, k, vt, batch, seq):
    qb = MLA_Q_BLOCK
    nq = seq // qb
    t = batch * seq
    return pl.pallas_call(
        functools.partial(_mla_kernel, n_chunks=seq // MLA_KV_CHUNK),
        grid=(batch, MLA_HEADS, nq),
        in_specs=[
            pl.BlockSpec((qb, LANES), lambda b, h, i: (b * nq + i, h)),
            pl.BlockSpec((seq, LANES), lambda b, h, i: (b, h)),
            pl.BlockSpec((MLA_V, seq), lambda b, h, i: (h, b)),
        ],
        out_specs=pl.BlockSpec((MLA_V, qb), lambda b, h, i: (h, b * nq + i)),
        out_shape=jax.ShapeDtypeStruct((MLA_HEADS * MLA_V, t), BF16),
        compiler_params=_params(3),
        name="mla_attention",
    )(q, k, vt)


def _gqa_kernel(sink_ref, q_ref, kp_ref, kc_ref, kn_ref, kswp_ref, kswc_ref,
                kswn_ref, vp_ref, vc_ref, vn_ref, o_ref, *, seq):
    i = pl.program_id(1)
    qb = q_ref.shape[0]
    w = qb + 2 * WINDOW
    kw = jnp.concatenate([kp_ref[...], kc_ref[...], kn_ref[...]], axis=0)
    kw_sw = jnp.concatenate([kswp_ref[...], kswc_ref[...], kswn_ref[...]], axis=0)
    vw = jnp.concatenate([vp_ref[...], vc_ref[...], vn_ref[...]], axis=0)

    r = lax.broadcasted_iota(jnp.int32, (qb, w), 0)
    c = lax.broadcasted_iota(jnp.int32, (qb, w), 1)
    kpos = i * qb - WINDOW + c
    valid = (jnp.abs(r - c + WINDOW) <= WINDOW) & (kpos >= 0) & (kpos < seq)
    lane = lax.broadcasted_iota(jnp.int32, (qb, LANES), 1)
    low = lane < GQA_DIM

    def head(qh, keys, sink):
        s = jnp.where(valid, _dot_nt(qh, keys), NEG_INF)
        m = jnp.maximum(jnp.max(s, axis=-1, keepdims=True), sink)
        p = jnp.exp(s - m)
        den = jnp.sum(p, axis=-1, keepdims=True) + jnp.exp(sink - m)
        return p.astype(BF16), 1.0 / den

    group = GQA_HEADS // GQA_KV_HEADS
    for j in range(GQA_HEADS // 2):
        g = (2 * j) // group
        qp = q_ref[:, j * LANES:(j + 1) * LANES].astype(F32)
        q_lo = jnp.where(low, qp, 0.0).astype(BF16)
        q_hi = jnp.where(low, 0.0, qp).astype(BF16)
        k_lo, k_hi = (kw, kw_sw) if g == 0 else (kw_sw, kw)
        p_lo, inv_lo = head(q_lo, k_lo, sink_ref[2 * j])
        p_hi, inv_hi = head(q_hi, k_hi, sink_ref[2 * j + 1])
        v_lo = vw[:, (2 * g) * LANES:(2 * g + 1) * LANES]
        v_hi = vw[:, (2 * g + 1) * LANES:(2 * g + 2) * LANES]
        o = _dot(p_lo, v_lo) * inv_lo + _dot(p_hi, v_hi) * inv_hi
        o_ref[:, j * LANES:(j + 1) * LANES] = o.astype(BF16)


def _gqa_attention(sink, gq, gk, gksw, gv, batch, seq):
    qb = GQA_Q_BLOCK
    nq = seq // qb
    sb = seq // WINDOW
    r = qb // WINDOW
    t = batch * seq

    def prev_map(b, i):
        return (jnp.maximum(b * sb + i * r - 1, b * sb), 0)

    def next_map(b, i):
        return (jnp.minimum(b * sb + (i + 1) * r, b * sb + sb - 1), 0)

    def cur_map(b, i):
        return (b * nq + i, 0)

    def halo(width):
        return [pl.BlockSpec((WINDOW, width), prev_map),
                pl.BlockSpec((qb, width), cur_map),
                pl.BlockSpec((WINDOW, width), next_map)]

    in_specs = ([pl.BlockSpec(memory_space=pltpu.SMEM),
                 pl.BlockSpec((qb, GQA_HEADS * GQA_DIM), cur_map)]
                + halo(LANES) + halo(LANES) + halo(4 * LANES))
    return pl.pallas_call(
        functools.partial(_gqa_kernel, seq=seq),
        grid=(batch, nq),
        in_specs=in_specs,
        out_specs=pl.BlockSpec((qb, GQA_HEADS * GQA_DIM), cur_map),
        out_shape=jax.ShapeDtypeStruct((t, GQA_HEADS * GQA_DIM), BF16),
        compiler_params=_params(2),
        name="gqa_attention",
    )(sink, gq, gk, gk, gk, gksw, gksw, gksw, gv, gv, gv)


def _mix_out_kernel(x_ref, g_ref, wgate_ref, ot_ref, ogqa_ref, woa_ref, wob_ref,
                    wout_ref, o_ref):
    x = x_ref[...]
    u = _rmsnorm(x, g_ref[...]).astype(BF16)
    gates = _dot(u, wgate_ref[...])
    a = _dot_tn(ot_ref[...], woa_ref[...])
    b = _dot(ogqa_ref[...], wob_ref[...])
    merged = (jax.nn.sigmoid(gates[:, :D_MODEL]) * a
              + jax.nn.sigmoid(gates[:, D_MODEL:]) * b)
    o_ref[...] = x + _dot(merged.astype(BF16), wout_ref[...])


def _mix_out(x, lw, ot_mla, o_gqa):
    t = x.shape[0]
    tm = TOKEN_TILE
    tok = lambda w: pl.BlockSpec((tm, w), lambda i: (i, 0))
    hv = MLA_HEADS * MLA_V
    hd = GQA_HEADS * GQA_DIM
    return pl.pallas_call(
        _mix_out_kernel,
        grid=(t // tm,),
        in_specs=[
            tok(D_MODEL), _const_spec((1, D_MODEL)),
            _const_spec((D_MODEL, 2 * D_MODEL)),
            pl.BlockSpec((hv, tm), lambda i: (0, i)), tok(hd),
            _const_spec((hv, D_MODEL)), _const_spec((hd, D_MODEL)),
            _const_spec((D_MODEL, D_MODEL)),
        ],
        out_specs=tok(D_MODEL),
        out_shape=jax.ShapeDtypeStruct((t, D_MODEL), F32),
        compiler_params=_params(1),
        name="mix_out",
    )(x, lw["mix_norm"], lw["w_gate"], ot_mla, o_gqa, lw["w_o_mla"],
      lw["w_o_gqa"], lw["w_out"])


def _rope_tables(seq):
    pos = jnp.arange(seq, dtype=F32)[:, None]

    def cos_sin(d):
        freqs = ROPE_THETA ** (-jnp.arange(0, d, 2, dtype=F32) / d)
        ang = pos * freqs[None, :]
        return jnp.cos(ang), jnp.sin(ang)

    cm, sm = cos_sin(MLA_ROPE)
    one = jnp.ones((seq, MLA_NOPE), F32)
    zero = jnp.zeros((seq, MLA_NOPE), F32)
    pad1 = jnp.ones((seq, LANES - MLA_NOPE - MLA_ROPE), F32)
    pad0 = jnp.zeros_like(pad1)
    z16 = jnp.zeros_like(sm)
    c_m = jnp.concatenate([one, cm, cm, pad1], axis=1)
    s1_m = jnp.concatenate([zero, -sm, z16, pad0], axis=1)
    s2_m = jnp.concatenate([zero, z16, sm, pad0], axis=1)

    cg, sg = cos_sin(GQA_DIM)
    z32 = jnp.zeros_like(sg)
    c_g = jnp.concatenate([cg, cg, cg, cg], axis=1)
    s1_g = jnp.concatenate([-sg, z32, -sg, z32], axis=1)
    s2_g = jnp.concatenate([z32, sg, z32, sg], axis=1)
    return jnp.stack([c_m, s1_m, s2_m, c_g, s1_g, s2_g])


def _layer_weights(l, p):
    d = D_MODEL
    w_in = p["w_in"][l]
    cuts = np.cumsum([Q_LORA, KV_LORA, MLA_ROPE, GQA_HEADS * GQA_DIM,
                      GQA_KV_HEADS * GQA_DIM, GQA_KV_HEADS * GQA_DIM, d, d])
    w_cq = w_in[:, :cuts[0]]
    w_ckv = w_in[:, cuts[0]:cuts[1]]
    w_kpe = w_in[:, cuts[1]:cuts[2]]
    w_gq = w_in[:, cuts[2]:cuts[3]]
    w_gk = w_in[:, cuts[3]:cuts[4]]
    w_gv = w_in[:, cuts[4]:cuts[5]]
    w_gate = w_in[:, cuts[5]:cuts[7]]
    z = lambda n: jnp.zeros((d, n), F32)
    k0, k1 = w_gk[:, :GQA_DIM], w_gk[:, GQA_DIM:]
    v0, v1 = w_gv[:, :GQA_DIM], w_gv[:, GQA_DIM:]
    w_mix = jnp.concatenate([
        w_cq, w_ckv,
        z(MLA_NOPE), w_kpe, z(LANES - MLA_NOPE - MLA_ROPE),
        w_gq, k0, k1, k1, k0,
        v0, z(GQA_DIM), z(GQA_DIM), v0, v1, z(GQA_DIM), z(GQA_DIM), v1,
    ], axis=1)

    hd = MLA_NOPE + MLA_ROPE
    w_uq = p["w_uq"][l].reshape(Q_LORA, MLA_HEADS, hd)
    w_uq = jnp.pad(w_uq, ((0, 0), (0, 0), (0, LANES - hd)))
    w_ukv = p["w_ukv"][l].reshape(KV_LORA, MLA_HEADS, MLA_NOPE + MLA_V)
    w_uk = jnp.pad(w_ukv[:, :, :MLA_NOPE], ((0, 0), (0, 0), (0, LANES - MLA_NOPE)))
    w_uvt = w_ukv[:, :, MLA_NOPE:].reshape(KV_LORA, MLA_HEADS * MLA_V).T

    row = lambda a: a[l].reshape(1, -1)
    return {
        "ffn1_norm": row(p["ffn1_norm"]), "ffn2_norm": row(p["ffn2_norm"]),
        "mix_norm": row(p["mix_norm"]), "q_norm": row(p["q_norm"]),
        "kv_norm": row(p["kv_norm"]),
        "ffn1": tuple(p[n][l].astype(BF16) for n in ("ffn1_w1", "ffn1_w3", "ffn1_w2")),
        "ffn2": tuple(p[n][l].astype(BF16) for n in ("ffn2_w1", "ffn2_w3", "ffn2_w2")),
        "w_mix": w_mix.astype(BF16),
        "w_gate": w_gate.astype(BF16),
        "w_uq": w_uq.reshape(Q_LORA, MLA_HEADS * LANES).astype(BF16),
        "w_uk": w_uk.reshape(KV_LORA, MLA_HEADS * LANES).astype(BF16),
        "w_uvt": w_uvt.astype(BF16),
        "w_o_mla": p["w_o_mla"][l].astype(BF16),
        "w_o_gqa": p["w_o_gqa"][l].astype(BF16),
        "w_out": p["w_out"][l].astype(BF16),
        "sink": p["sink"][l],
    }


def _trunk(x3, layers, tables, final_gain):
    batch, seq, _ = x3.shape
    x = x3.reshape(batch * seq, D_MODEL)
    for l, lw in enumerate(layers):
        x = _ffn(x, lw["ffn1_norm"], *lw["ffn1"])
        q, k, vt, gq, gk, gksw, gv = _mix_in(x, lw, tables, seq)
        ot_mla = _mla_attention(q, k, vt, batch, seq)
        o_gqa = _gqa_attention(lw["sink"], gq, gk, gksw, gv, batch, seq)
        x = _mix_out(x, lw, ot_mla, o_gqa)
        x = _ffn(x, lw["ffn2_norm"], *lw["ffn2"],
                 final_gain=final_gain if l == DEPTH - 1 else None)
    return x.reshape(batch, seq, D_MODEL)


def kernel(x_prompt, x_sample, ffn1_norm, ffn1_w1, ffn1_w3, ffn1_w2, mix_norm, w_in, q_norm, kv_norm, w_uq, w_ukv, w_o_mla, sink, w_o_gqa, w_out, ffn2_norm, ffn2_w1, ffn2_w3, ffn2_w2, final_norm):
    p = dict(ffn1_norm=ffn1_norm, ffn1_w1=ffn1_w1, ffn1_w3=ffn1_w3, ffn1_w2=ffn1_w2,
             mix_norm=mix_norm, w_in=w_in, q_norm=q_norm, kv_norm=kv_norm,
             w_uq=w_uq, w_ukv=w_ukv, w_o_mla=w_o_mla, sink=sink, w_o_gqa=w_o_gqa,
             w_out=w_out, ffn2_norm=ffn2_norm, ffn2_w1=ffn2_w1, ffn2_w3=ffn2_w3,
             ffn2_w2=ffn2_w2)
    layers = [_layer_weights(l, p) for l in range(DEPTH)]
    tables = _rope_tables(max(x_prompt.shape[1], x_sample.shape[1]))
    final_gain = final_norm.reshape(1, D_MODEL)
    y_prompt = _trunk(x_prompt, layers, tables, final_gain)
    y_sample = _trunk(x_sample, layers, tables, final_gain)
    return (y_prompt, y_sample)
```
